```python
import jax, jax.numpy as jnp
from jax import lax
import numpy as np

D_MODEL = 1024
BATCH = 8
SEQ = 2048
DEPTH = 2

POOL_WINDOWS = (2, 4, 8, 16)
N_POOL_GROUPS = len(POOL_WINDOWS)
POOL_WIDTH = D_MODEL // 2
POOL_GROUP = POOL_WIDTH // N_POOL_GROUPS
POOL_OUT_GROUP = D_MODEL // N_POOL_GROUPS
ATTN_HEADS = 8
HEAD_DIM = 64
ATTN_WIDTH = ATTN_HEADS * HEAD_DIM
MOBA_BLOCK = 256
MOBA_TOPK = 3
Q_CHUNK = 16
N_BRANCHES = 2
IN_WIDTH = POOL_WIDTH + 3 * ATTN_WIDTH + N_BRANCHES * D_MODEL
PEER_HEADS = 8
PEER_NKEYS = 128
PEER_EXPERTS = PEER_NKEYS * PEER_NKEYS
PEER_QDIM = 256
PEER_HALF = PEER_QDIM // 2
PEER_TOPK = 16
TOKEN_CHUNK = 128
EPS = 1e-6

kernel_name = "hybrid_pool_moba_peer_trunk"


def rmsnorm(x, g):
    xf = x.astype(jnp.float32)
    y = xf * lax.rsqrt(jnp.mean(xf * xf, axis=-1, keepdims=True) + EPS)
    return (y * g.astype(jnp.float32)).astype(x.dtype)


def alibi_slopes(n_heads):
    return jnp.asarray([2.0 ** (-8.0 * (h + 1) / n_heads) for h in range(n_heads)], jnp.float32)


def pool_mixer(p, w_pool, pool_scale):
    B_, S_, _ = p.shape
    pf = p.astype(jnp.float32)
    c = jnp.pad(jnp.cumsum(pf, axis=1), ((0, 0), (1, 0), (0, 0)))
    pos = jnp.arange(S_)
    outs = []
    for gi, w in enumerate(POOL_WINDOWS):
        cg = c[..., gi * POOL_GROUP:(gi + 1) * POOL_GROUP]
        start = jnp.maximum(pos + 1 - w, 0)
        cnt = (pos + 1 - start).astype(jnp.float32)
        win_mean = (cg[:, 1:] - jnp.take(cg, start, axis=1)) / cnt[None, :, None]
        outs.append(win_mean - pf[..., gi * POOL_GROUP:(gi + 1) * POOL_GROUP])
    m = jnp.stack(outs, axis=2)
    y = jnp.einsum('bsgc,gco->bsgo', m, w_pool.astype(jnp.float32)).reshape(B_, S_, D_MODEL)
    return (y * pool_scale.astype(jnp.float32)).astype(p.dtype)


def moba_attention(q, k, v):
    B_, H_, S_, dh = q.shape
    n_blk = -(-S_ // MOBA_BLOCK)
    pad = n_blk * MOBA_BLOCK - S_
    kp = jnp.pad(k, ((0, 0), (0, 0), (0, pad), (0, 0)))
    vp = jnp.pad(v, ((0, 0), (0, 0), (0, pad), (0, 0)))
    k_blocks = kp.reshape(B_, H_, n_blk, MOBA_BLOCK, dh)
    v_blocks = vp.reshape(B_, H_, n_blk, MOBA_BLOCK, dh)
    k_mean = jnp.mean(k_blocks.astype(jnp.float32), axis=3)
    top = min(MOBA_TOPK, n_blk)
    slopes = alibi_slopes(H_)
    scale = dh ** -0.5
    b_ix = jnp.arange(B_)[:, None, None, None]
    h_ix = jnp.arange(H_)[None, :, None, None]
    blk_ids = jnp.arange(n_blk)
    offs = jnp.arange(MOBA_BLOCK)

    def chunk(ci):
        t0 = ci * Q_CHUNK
        qc = lax.dynamic_slice_in_dim(q, t0, Q_CHUNK, axis=2).astype(jnp.float32)
        t = t0 + jnp.arange(Q_CHUNK)
        own = t0 // MOBA_BLOCK
        bscore = jnp.einsum('bhcd,bhnd->bhcn', qc, k_mean)
        bscore = jnp.where(blk_ids < own, bscore, -jnp.inf)
        _, sel = lax.top_k(bscore, top)
        sel_ok = sel < own
        kg = k_blocks[b_ix, h_ix, sel].astype(jnp.float32)
        vg = v_blocks[b_ix, h_ix, sel].astype(jnp.float32)
        s_sel = jnp.einsum('bhcd,bhcjkd->bhcjk', qc, kg) * scale
        dist_sel = t[None, None, :, None, None] - (sel[..., None] * MOBA_BLOCK + offs)
        s_sel = s_sel - slopes[None, :, None, None, None] * dist_sel
        s_sel = jnp.where(sel_ok[..., None], s_sel, -jnp.inf)
        k_own = lax.dynamic_slice_in_dim(kp, own * MOBA_BLOCK, MOBA_BLOCK, axis=2).astype(jnp.float32)
        v_own = lax.dynamic_slice_in_dim(vp, own * MOBA_BLOCK, MOBA_BLOCK, axis=2).astype(jnp.float32)
        dist_own = t[:, None] - (own * MOBA_BLOCK + offs)[None, :]
        s_own = jnp.einsum('bhcd,bhkd->bhck', qc, k_own) * scale
        s_own = s_own - slopes[None, :, None, None] * dist_own
        s_own = jnp.where(dist_own >= 0, s_own, -jnp.inf)
        s_all = jnp.concatenate([s_sel.reshape(B_, H_, Q_CHUNK, top * MOBA_BLOCK), s_own], axis=-1)
        pr = jax.nn.softmax(s_all, axis=-1)
        p_sel = pr[..., :top * MOBA_BLOCK].reshape(B_, H_, Q_CHUNK, top, MOBA_BLOCK)
        p_own = pr[..., top * MOBA_BLOCK:]
        o = (jnp.einsum('bhcjk,bhcjkd->bhcd', p_sel, vg)
             + jnp.einsum('bhck,bhkd->bhcd', p_own, v_own))
        return o.astype(q.dtype)

    out = lax.map(chunk, jnp.arange(S_ // Q_CHUNK))
    return out.transpose(1, 2, 0, 3, 4).reshape(B_, H_, S_, dh)


def peer_ffn(x, w_q, sub_keys, u, v):
    B_, S_, D_ = x.shape
    xt = x.reshape((B_ * S_) // TOKEN_CHUNK, TOKEN_CHUNK, D_)

    def chunk(xc):
        q = (xc @ w_q).reshape(TOKEN_CHUNK, PEER_HEADS, 2, PEER_HALF).astype(jnp.float32)
        s = jnp.einsum('thpd,pkd->thpk', q, sub_keys.astype(jnp.float32))
        sv, si = lax.top_k(s, PEER_TOPK)
        cand = sv[:, :, 0, :, None] + sv[:, :, 1, None, :]
        cand_id = si[:, :, 0, :, None] * PEER_NKEYS + si[:, :, 1, None, :]
        cv, ci = lax.top_k(cand.reshape(TOKEN_CHUNK, PEER_HEADS, PEER_TOPK * PEER_TOPK), PEER_TOPK)
        eid = jnp.take_along_axis(cand_id.reshape(TOKEN_CHUNK, PEER_HEADS, PEER_TOPK * PEER_TOPK), ci, axis=-1)
        g = jax.nn.softmax(cv, axis=-1)
        ug = u[eid].astype(jnp.float32)
        vg = v[eid].astype(jnp.float32)
        act = jax.nn.gelu(jnp.einsum('thkd,td->thk', ug, xc.astype(jnp.float32)))
        y = jnp.einsum('thk,thkd->td', g * act, vg)
        return y.astype(x.dtype)

    return lax.map(chunk, xt).reshape(B_, S_, D_)


def hybrid_layer(x, g_mix, w_in, w_pool, pool_scale, w_attn_up, w_out, g_ffn, w_peer_q, peer_keys, peer_u, peer_v):
    B_, S_, _ = x.shape
    h = rmsnorm(x, g_mix)
    z = h @ w_in
    o1 = POOL_WIDTH
    o2 = o1 + ATTN_WIDTH
    o3 = o2 + ATTN_WIDTH
    o4 = o3 + ATTN_WIDTH
    p, q, k, v, gl = z[..., :o1], z[..., o1:o2], z[..., o2:o3], z[..., o3:o4], z[..., o4:]
    branch_a = pool_mixer(p, w_pool, pool_scale)
    to_heads = lambda t: t.reshape(B_, S_, ATTN_HEADS, HEAD_DIM).transpose(0, 2, 1, 3)
    o = moba_attention(to_heads(q), to_heads(k), to_heads(v))
    branch_b = o.transpose(0, 2, 1, 3).reshape(B_, S_, ATTN_WIDTH) @ w_attn_up
    gates = jax.nn.sigmoid(gl.astype(jnp.float32)).astype(x.dtype)
    merged = gates[..., :D_MODEL] * branch_a + gates[..., D_MODEL:] * branch_b
    x = x + merged @ w_out
    x = x + peer_ffn(rmsnorm(x, g_ffn), w_peer_q, peer_keys, peer_u, peer_v)
    return x


def setup_inputs(seed: int = 0) -> dict:
    key = jax.random.key(seed)
    ks = jax.random.split(key, 14)
    f = jnp.float32
    nrm = lambda k_, shape, s: jax.random.normal(k_, shape, f) * s
    return {
        "x": nrm(ks[0], (BATCH, SEQ, D_MODEL), 1.0),
        "mix_norm": 1.0 + nrm(ks[1], (DEPTH, D_MODEL), 0.02),
        "w_in": nrm(ks[2], (DEPTH, D_MODEL, IN_WIDTH), D_MODEL ** -0.5),
        "w_pool": nrm(ks[3], (DEPTH, N_POOL_GROUPS, POOL_GROUP, POOL_OUT_GROUP), POOL_GROUP ** -0.5),
        "pool_scale": 1.0 + nrm(ks[4], (DEPTH, D_MODEL), 0.02),
        "w_attn_up": nrm(ks[5], (DEPTH, ATTN_WIDTH, D_MODEL), ATTN_WIDTH ** -0.5),
        "w_out": nrm(ks[6], (DEPTH, D_MODEL, D_MODEL), D_MODEL ** -0.5),
        "ffn_norm": 1.0 + nrm(ks[7], (DEPTH, D_MODEL), 0.02),
        "peer_wq": nrm(ks[8], (DEPTH, D_MODEL, PEER_HEADS * PEER_QDIM), D_MODEL ** -0.5),
        "peer_keys": nrm(ks[9], (DEPTH, 2, PEER_NKEYS, PEER_HALF), PEER_HALF ** -0.5),
        "peer_u": nrm(ks[10], (DEPTH, PEER_EXPERTS, D_MODEL), D_MODEL ** -0.5),
        "peer_v": nrm(ks[11], (DEPTH, PEER_EXPERTS, D_MODEL), PEER_HEADS ** -0.5),
        "final_norm": 1.0 + nrm(ks[12], (D_MODEL,), 0.02),
    }


def reference(x, mix_norm, w_in, w_pool, pool_scale, w_attn_up, w_out, ffn_norm, peer_wq, peer_keys, peer_u, peer_v, final_norm):
    for l in range(DEPTH):
        x = hybrid_layer(x, mix_norm[l], w_in[l], w_pool[l], pool_scale[l], w_attn_up[l], w_out[l],
                         ffn_norm[l], peer_wq[l], peer_keys[l], peer_u[l], peer_v[l])
    return rmsnorm(x, final_norm)
```

```python
import functools

import jax
import jax.numpy as jnp
import numpy as np
from jax import lax
from jax.experimental import pallas as pl
from jax.experimental.pallas import tpu as pltpu

F32 = jnp.float32
BF16 = jnp.bfloat16

EPS = 1e-6
POOL_WINDOWS = (2, 4, 8, 16)
POOL_GROUP = 128
POOL_OUT_GROUP = 256
POOL_WIDTH = 512
POOL_HALO = 16
N_HEADS = 8
HEAD_DIM = 64
ATTN_WIDTH = N_HEADS * HEAD_DIM
MOBA_BLOCK = 256
MOBA_TOPK = 3
PEER_HEADS = 8
PEER_NKEYS = 128
PEER_HALF = 128
PEER_TOPK = 16
PEER_SLOTS = PEER_HEADS * PEER_TOPK
MASKED = -1e30

VMEM_LIMIT_BYTES = 56 * 1024 * 1024

IN_PROJ_ROWS = 512
ROUTE_ROWS = 256
EXPERT_ROWS = 256
EXPERT_TILE = 2048
W_PITCH = EXPERT_ROWS + 8

CAND_ROWS = 80


def _rms(x, g):
    return x * lax.rsqrt(jnp.mean(x * x, axis=-1, keepdims=True) + EPS) * g


def _params(*sem):
    return pltpu.CompilerParams(dimension_semantics=sem, vmem_limit_bytes=VMEM_LIMIT_BYTES)


def _in_proj_kernel(x_ref, g_ref, wpkv_ref, wqt_ref, wg_ref,
                    p_ref, qt_ref, k_ref, v_ref, gate_ref, kmean_ref):
    h = _rms(x_ref[...], g_ref[...]).astype(BF16)
    pkv = jnp.dot(h, wpkv_ref[...], preferred_element_type=F32)
    p_ref[...] = pkv[:, :POOL_WIDTH]
    k = pkv[:, POOL_WIDTH:POOL_WIDTH + ATTN_WIDTH]
    k_ref[...] = k.astype(BF16)
    v_ref[...] = pkv[:, POOL_WIDTH + ATTN_WIDTH:].astype(BF16)
    nb = k.shape[0] // MOBA_BLOCK
    kmean_ref[...] = (jnp.sum(k.reshape(nb, MOBA_BLOCK, ATTN_WIDTH), axis=1) / MOBA_BLOCK
                      ).reshape(nb, 1, ATTN_WIDTH)
    qt = lax.dot_general(wqt_ref[...], h, (((1,), (1,)), ((), ())), preferred_element_type=F32)
    qt_ref[...] = (qt * (HEAD_DIM ** -0.5)).astype(BF16)
    gl = jnp.dot(h, wg_ref[...], preferred_element_type=F32)
    gate_ref[...] = (1.0 / (1.0 + jnp.exp(-gl))).astype(BF16)


def _in_proj(x, g, wpkv, wqt, wg):
    n, d = x.shape
    rows = IN_PROJ_ROWS
    nb = rows // MOBA_BLOCK
    const = lambda t: (0, 0)
    return pl.pallas_call(
        _in_proj_kernel,
        grid=(n // rows,),
        in_specs=[
            pl.BlockSpec((rows, d), lambda t: (t, 0)),
            pl.BlockSpec((1, d), const),
            pl.BlockSpec(wpkv.shape, const),
            pl.BlockSpec(wqt.shape, const),
            pl.BlockSpec(wg.shape, const),
        ],
        out_specs=[
            pl.BlockSpec((rows, POOL_WIDTH), lambda t: (t, 0)),
            pl.BlockSpec((ATTN_WIDTH, rows), lambda t: (0, t)),
            pl.BlockSpec((rows, ATTN_WIDTH), lambda t: (t, 0)),
            pl.BlockSpec((rows, ATTN_WIDTH), lambda t: (t, 0)),
            pl.BlockSpec((rows, wg.shape[1]), lambda t: (t, 0)),
            pl.BlockSpec((nb, 1, ATTN_WIDTH), lambda t: (t, 0, 0)),
        ],
        out_shape=[
            jax.ShapeDtypeStruct((n, POOL_WIDTH), F32),
            jax.ShapeDtypeStruct((ATTN_WIDTH, n), BF16),
            jax.ShapeDtypeStruct((n, ATTN_WIDTH), BF16),
            jax.ShapeDtypeStruct((n, ATTN_WIDTH), BF16),
            jax.ShapeDtypeStruct((n, wg.shape[1]), BF16),
            jax.ShapeDtypeStruct((n // MOBA_BLOCK, 1, ATTN_WIDTH), F32),
        ],
        compiler_params=_params("parallel"),
        name="in_proj",
    )(x, g, wpkv, wqt, wg)


def _mixer_kernel(qt_ref, k_ref, v_ref, kmean_ref, p_ref, pprev_ref, gate_ref, x_ref,
                  wpool_ref, pscale_ref, wup_ref, wout_ref, o_ref,
                  sel_s, m_s, l_s, acc_s, pwin_s):
    blk = MOBA_BLOCK
    i = pl.program_id(1)
    n_blocks = kmean_ref.shape[1]
    slopes = [2.0 ** (-8.0 * (h + 1) / N_HEADS) for h in range(N_HEADS)]

    kmean = kmean_ref[0]
    jrow = lax.broadcasted_iota(jnp.int32, (n_blocks, blk), 0)
    for h in range(N_HEADS):
        hs = slice(h * HEAD_DIM, (h + 1) * HEAD_DIM)
        bs = jnp.dot(kmean[:, hs].astype(BF16), qt_ref[hs, :], preferred_element_type=F32)
        sel = jnp.zeros((n_blocks, blk), F32)
        for j0 in range(n_blocks):
            ref_row = bs[j0:j0 + 1, :]
            ahead = ((bs > ref_row) | ((bs == ref_row) & (jrow < j0))) & (jrow < i)
            cnt = jnp.sum(ahead.astype(F32), axis=0, keepdims=True)
            ok = (cnt < MOBA_TOPK) & (j0 < i)
            sel = jnp.where(jrow == j0, jnp.where(ok, 0.0, MASKED), sel)
        sel_s[h * n_blocks:(h + 1) * n_blocks, :] = sel

    dist0 = (lax.broadcasted_iota(jnp.int32, (blk, blk), 1)
             - lax.broadcasted_iota(jnp.int32, (blk, blk), 0)).astype(F32)

    own = pl.multiple_of(i * blk, blk)
    k_own = k_ref[pl.ds(own, blk), :]
    v_own = v_ref[pl.ds(own, blk), :]
    for h in range(N_HEADS):
        hs = slice(h * HEAD_DIM, (h + 1) * HEAD_DIM)
        s = jnp.dot(k_own[:, hs], qt_ref[hs, :], preferred_element_type=F32)
        s = jnp.where(dist0 >= 0, s - slopes[h] * dist0, MASKED)
        m = jnp.max(s, axis=0, keepdims=True)
        e = jnp.exp(s - m)
        m_s[h:h + 1, :] = m
        l_s[h:h + 1, :] = jnp.sum(e, axis=0, keepdims=True)
        acc_s[hs, :] = lax.dot_general(v_own[:, hs], e.astype(BF16), (((0,), (0,)), ((), ())),
                                       preferred_element_type=F32)

    def past(j, carry):
        start = pl.multiple_of(j * blk, blk)
        kj = k_ref[pl.ds(start, blk), :]
        vj = v_ref[pl.ds(start, blk), :]
        gap = ((i - j) * blk).astype(F32)
        for h in range(N_HEADS):
            hs = slice(h * HEAD_DIM, (h + 1) * HEAD_DIM)
            s = jnp.dot(kj[:, hs], qt_ref[hs, :], preferred_element_type=F32)
            row_bias = sel_s[pl.ds(h * n_blocks + j, 1), :] - slopes[h] * gap
            s = s - slopes[h] * dist0 + row_bias
            m_old = m_s[h:h + 1, :]
            m_new = jnp.maximum(m_old, jnp.max(s, axis=0, keepdims=True))
            alpha = jnp.exp(m_old - m_new)
            e = jnp.exp(s - m_new)
            m_s[h:h + 1, :] = m_new
            l_s[h:h + 1, :] = alpha * l_s[h:h + 1, :] + jnp.sum(e, axis=0, keepdims=True)
            acc_s[hs, :] = alpha * acc_s[hs, :] + lax.dot_general(
                vj[:, hs], e.astype(BF16), (((0,), (0,)), ((), ())), preferred_element_type=F32)
        return carry

    lax.fori_loop(0, i, past, 0)

    for h in range(N_HEADS):
        hs = slice(h * HEAD_DIM, (h + 1) * HEAD_DIM)
        acc_s[hs, :] = acc_s[hs, :] / l_s[h:h + 1, :]
    attn = acc_s[...].T.astype(BF16)
    branch_b = jnp.dot(attn, wup_ref[...], preferred_element_type=F32)

    p = p_ref[...]
    pwin_s[POOL_HALO:, :] = p
    pwin_s[:POOL_HALO, :] = jnp.where(i > 0, pprev_ref[...], 0.0)
    pos = lax.broadcasted_iota(jnp.int32, (blk, 1), 0) + i * blk
    pooled = []
    for gi, w in enumerate(POOL_WINDOWS):
        cs = slice(gi * POOL_GROUP, (gi + 1) * POOL_GROUP)
        tot = p[:, cs]
        for d in range(1, w):
            tot = tot + pwin_s[POOL_HALO - d:POOL_HALO - d + blk, cs]
        cnt = jnp.minimum(pos + 1, w).astype(F32)
        mdiff = tot / cnt - p[:, cs]
        pooled.append(jnp.dot(mdiff.astype(BF16), wpool_ref[gi], preferred_element_type=F32))
    branch_a = jnp.concatenate(pooled, axis=1) * pscale_ref[...]

    gates = gate_ref[...]
    d_model = branch_a.shape[1]
    merged = gates[:, :d_model].astype(F32) * branch_a + gates[:, d_model:].astype(F32) * branch_b
    o_ref[...] = x_ref[...] + jnp.dot(merged.astype(BF16), wout_ref[...], preferred_element_type=F32)


def _mixer(qt, k, v, kmean, p, gates, x, wpool, pscale, wup, wout, batch, seq):
    n, d = x.shape
    blk = MOBA_BLOCK
    nb = seq // blk
    halo_per_blk = blk // POOL_HALO
    row = lambda b, i: (b * nb + i, 0)
    const2 = lambda b, i: (0, 0)
    return pl.pallas_call(
        _mixer_kernel,
        grid=(batch, nb),
        in_specs=[
            pl.BlockSpec((ATTN_WIDTH, blk), lambda b, i: (0, b * nb + i)),
            pl.BlockSpec((seq, ATTN_WIDTH), lambda b, i: (b, 0)),
            pl.BlockSpec((seq, ATTN_WIDTH), lambda b, i: (b, 0)),
            pl.BlockSpec((1, nb, ATTN_WIDTH), lambda b, i: (b, 0, 0)),
            pl.BlockSpec((blk, POOL_WIDTH), row),
            pl.BlockSpec((POOL_HALO, POOL_WIDTH),
                         lambda b, i: (jnp.maximum((b * nb + i) * halo_per_blk - 1, 0), 0)),
            pl.BlockSpec((blk, gates.shape[1]), row),
            pl.BlockSpec((blk, d), row),
            pl.BlockSpec(wpool.shape, lambda b, i: (0, 0, 0)),
            pl.BlockSpec((1, d), const2),
            pl.BlockSpec(wup.shape, const2),
            pl.BlockSpec(wout.shape, const2),
        ],
        out_specs=pl.BlockSpec((blk, d), row),
        out_shape=jax.ShapeDtypeStruct((n, d), F32),
        scratch_shapes=[
            pltpu.VMEM((N_HEADS * nb, blk), F32),
            pltpu.VMEM((N_HEADS, blk), F32),
            pltpu.VMEM((N_HEADS, blk), F32),
            pltpu.VMEM((ATTN_WIDTH, blk), F32),
            pltpu.VMEM((POOL_HALO + blk, POOL_WIDTH), F32),
        ],
        compiler_params=_params("parallel", "arbitrary"),
        name="mixer",
    )(qt, k, v, kmean.reshape(batch, nb, ATTN_WIDTH), p, p, gates, x, wpool, pscale, wup, wout)


def _extract_top(vals, ids, count):
    rows = lax.broadcasted_iota(jnp.int32, (count, vals.shape[1]), 0)
    top_v = jnp.zeros((count, vals.shape[1]), F32)
    top_i = jnp.zeros((count, vals.shape[1]), F32)
    for r in range(count):
        m = jnp.max(vals, axis=0, keepdims=True)
        pick = jnp.min(jnp.where(vals == m, ids, 1e9), axis=0, keepdims=True)
        top_v = jnp.where(rows == r, m, top_v)
        top_i = jnp.where(rows == r, pick, top_i)
        vals = jnp.where(ids == pick, -jnp.inf, vals)
    return top_v, top_i


def _take_rows(table, idx):
    out = jnp.zeros(idx.shape, F32)
    for a in range(table.shape[0]):
        out = jnp.where(idx == a, table[a:a + 1, :], out)
    return out


def _peer_route_kernel(x_ref, g_ref, wq_ref, keys_ref, cid_ref, cok_ref,
                       hn_ref, i_ref, j_ref, gw_ref):
    hn = _rms(x_ref[...], g_ref[...]).astype(BF16)
    hn_ref[...] = hn
    q = jnp.dot(hn, wq_ref[...], preferred_element_type=F32)
    t = q.shape[0]
    key_ids = lax.broadcasted_iota(jnp.int32, (PEER_NKEYS, t), 0).astype(F32)
    cid = cid_ref[...]
    cok = cok_ref[...]
    i_rows, j_rows, g_rows = [], [], []
    for h in range(PEER_HEADS):
        sv, si = [], []
        for half in range(2):
            c = (h * 2 + half) * PEER_HALF
            s = lax.dot_general(keys_ref[half], q[:, c:c + PEER_HALF].astype(BF16),
                                (((1,), (1,)), ((), ())), preferred_element_type=F32)
            top_v, top_i = _extract_top(s, key_ids, PEER_TOPK)
            sv.append(top_v)
            si.append(top_i)
        pieces = [sv[0][0:1, :] + sv[1]]
        for a in range(1, 8):
            pieces.append(sv[0][a:a + 1, :] + sv[1][0:8, :])
        pieces.append(sv[0][8:16, :] + sv[1][0:1, :])
        cand = jnp.where(cok > 0, jnp.concatenate(pieces, axis=0), -jnp.inf)
        cv, cpick = _extract_top(cand, cid, PEER_TOPK)
        a_idx = jnp.floor(cpick * (1.0 / PEER_TOPK))
        b_idx = cpick - a_idx * PEER_TOPK
        i_rows.append(_take_rows(si[0], a_idx))
        j_rows.append(_take_rows(si[1], b_idx))
        e = jnp.exp(cv - cv[0:1, :])
        g_rows.append(e / jnp.sum(e, axis=0, keepdims=True))
    i_ref[...] = jnp.concatenate(i_rows, axis=0).T
    j_ref[...] = jnp.concatenate(j_rows, axis=0).T
    gw_ref[...] = jnp.concatenate(g_rows, axis=0).T


def _candidate_tables(t):
    cid = np.zeros((CAND_ROWS,), np.float32)
    ok = np.zeros((CAND_ROWS,), np.float32)
    r = 0
    for a in range(8):
        width = 16 if a == 0 else 8
        for b in range(width):
            cid[r] = a * PEER_TOPK + b
            ok[r] = 1.0 if (a + 1) * (b + 1) <= PEER_TOPK else 0.0
            r += 1
    for a in range(8, 16):
        cid[r] = a * PEER_TOPK
        ok[r] = 1.0
        r += 1
    assert r == CAND_ROWS
    tile = lambda c: jnp.asarray(np.ascontiguousarray(np.broadcast_to(c[:, None], (CAND_ROWS, t))))
    return tile(cid), tile(ok)


def _peer_route(x, g, wq, keys):
    n, d = x.shape
    rows = ROUTE_ROWS
    cid, cok = _candidate_tables(rows)
    const = lambda t: (0, 0)
    row = lambda t: (t, 0)
    return pl.pallas_call(
        _peer_route_kernel,
        grid=(n // rows,),
        in_specs=[
            pl.BlockSpec((rows, d), row),
            pl.BlockSpec((1, d), const),
            pl.BlockSpec(wq.shape, const),
            pl.BlockSpec(keys.shape, lambda t: (0, 0, 0)),
            pl.BlockSpec(cid.shape, const),
            pl.BlockSpec(cok.shape, const),
        ],
        out_specs=[
            pl.BlockSpec((rows, d), row),
            pl.BlockSpec((rows, PEER_SLOTS), row),
            pl.BlockSpec((rows, PEER_SLOTS), row),
            pl.BlockSpec((rows, PEER_SLOTS), row),
        ],
        out_shape=[
            jax.ShapeDtypeStruct((n, d), BF16),
            jax.ShapeDtypeStruct((n, PEER_SLOTS), F32),
            jax.ShapeDtypeStruct((n, PEER_SLOTS), F32),
            jax.ShapeDtypeStruct((n, PEER_SLOTS), F32),
        ],
        compiler_params=_params("parallel"),
        name="peer_route",
    )(x, g, wq, keys, cid, cok)


def _gelu_tanh(x):
    return 0.5 * x * (1.0 + jnp.tanh(0.7978845608028654 * (x + 0.044715 * (x * x * x))))


def _peer_experts_kernel(hn_ref, i_ref, j_ref, gw_ref, x_ref, ut_ref, v_ref, gfin_ref,
                         o_ref, w_s, *, final_norm):
    e = pl.program_id(1)
    rows = hn_ref.shape[0]
    rows_per_tile = ut_ref.shape[1] // PEER_NKEYS

    @pl.when(e == 0)
    def _build_weights():
        sub = lax.broadcasted_iota(jnp.int32, (PEER_NKEYS, PEER_SLOTS), 0).astype(F32)

        def token(t, carry):
            i_row = jnp.broadcast_to(i_ref[pl.ds(t, 1), :], (PEER_NKEYS, PEER_SLOTS))
            j_row = jnp.broadcast_to(j_ref[pl.ds(t, 1), :], (PEER_NKEYS, PEER_SLOTS))
            g_row = jnp.broadcast_to(gw_ref[pl.ds(t, 1), :], (PEER_NKEYS, PEER_SLOTS))
            first = jnp.where(sub == i_row, g_row, 0.0).astype(BF16)
            second = jnp.where(sub == j_row, 1.0, 0.0).astype(BF16)
            w = lax.dot_general(first, second, (((1,), (1,)), ((), ())),
                                preferred_element_type=F32)
            w_s[pl.ds(t, PEER_NKEYS, stride=W_PITCH), :] = w
            return carry

        lax.fori_loop(0, rows, token, 0)
        o_ref[...] = x_ref[...]

    act = _gelu_tanh(jnp.dot(hn_ref[...], ut_ref[...], preferred_element_type=F32))
    base = e * rows_per_tile
    w = jnp.concatenate(
        [w_s[pl.ds(pl.multiple_of((base + r) * W_PITCH, 8), rows), :] for r in range(rows_per_tile)],
        axis=1)
    o_ref[...] += jnp.dot((act * w).astype(BF16), v_ref[...], preferred_element_type=F32)

    if final_norm:
        @pl.when(e == pl.num_programs(1) - 1)
        def _finish():
            o_ref[...] = _rms(o_ref[...], gfin_ref[...])


def _peer_experts(hn, i_idx, j_idx, gw, x, ut, v, gfin, final_norm):
    n, d = x.shape
    rows = EXPERT_ROWS
    tile = EXPERT_TILE
    n_experts = v.shape[0]
    row = lambda t, e: (t, 0)
    return pl.pallas_call(
        functools.partial(_peer_experts_kernel, final_norm=final_norm),
        grid=(n // rows, n_experts // tile),
        in_specs=[
            pl.BlockSpec((rows, d), row),
            pl.BlockSpec((rows, PEER_SLOTS), row),
            pl.BlockSpec((rows, PEER_SLOTS), row),
            pl.BlockSpec((rows, PEER_SLOTS), row),
            pl.BlockSpec((rows, d), row),
            pl.BlockSpec((d, tile), lambda t, e: (0, e)),
            pl.BlockSpec((tile, d), lambda t, e: (e, 0)),
            pl.BlockSpec((1, d), lambda t, e: (0, 0)),
        ],
        out_specs=pl.BlockSpec((rows, d), row),
        out_shape=jax.ShapeDtypeStruct((n, d), F32),
        scratch_shapes=[pltpu.VMEM((PEER_NKEYS * W_PITCH, PEER_NKEYS), F32)],
        compiler_params=_params("parallel", "arbitrary"),
        name="peer_experts",
    )(hn, i_idx, j_idx, gw, x, ut, v, gfin)


def kernel(x, mix_norm, w_in, w_pool, pool_scale, w_attn_up, w_out, ffn_norm, peer_wq, peer_keys,
           peer_u, peer_v, final_norm):
    batch, seq, d = x.shape
    depth = w_in.shape[0]
    assert seq % MOBA_BLOCK == 0 and (batch * seq) % IN_PROJ_ROWS == 0
    xf = x.reshape(batch * seq, d)
    o_q, o_k, o_v, o_g = POOL_WIDTH, POOL_WIDTH + ATTN_WIDTH, POOL_WIDTH + 2 * ATTN_WIDTH, \
        POOL_WIDTH + 3 * ATTN_WIDTH
    for l in range(depth):
        wl = w_in[l]
        wpkv = jnp.concatenate([wl[:, :o_q], wl[:, o_k:o_g]], axis=1).astype(BF16)
        wqt = wl[:, o_q:o_k].T.astype(BF16)
        wg = wl[:, o_g:].astype(BF16)
        p, qt, k, v, gates, kmean = _in_proj(xf, mix_norm[l][None, :], wpkv, wqt, wg)
        xf = _mixer(qt, k, v, kmean, p, gates, xf, w_pool[l].astype(BF16), pool_scale[l][None, :],
                    w_attn_up[l].astype(BF16), w_out[l].astype(BF16), batch, seq)
        hn, i_idx, j_idx, gw = _peer_route(xf, ffn_norm[l][None, :], peer_wq[l].astype(BF16),
                                           peer_keys[l].astype(BF16))
        xf = _peer_experts(hn, i_idx, j_idx, gw, xf, peer_u[l].T.astype(BF16),
                           peer_v[l].astype(BF16), final_norm[None, :], l == depth - 1)
    return xf.reshape(batch, seq, d)
```

```python
import functools

import jax
import jax.numpy as jnp
import numpy as np
from jax import lax
from jax.experimental import pallas as pl
from jax.experimental.pallas import tpu as pltpu

F32 = jnp.float32
BF16 = jnp.bfloat16

EPS = 1e-6
POOL_WINDOWS = (2, 4, 8, 16)
POOL_GROUP = 128
POOL_OUT_GROUP = 256
POOL_WIDTH = 512
POOL_HALO = 16
N_HEADS = 8
HEAD_DIM = 64
ATTN_WIDTH = N_HEADS * HEAD_DIM
MOBA_BLOCK = 256
MOBA_TOPK = 3
PEER_HEADS = 8
PEER_NKEYS = 128
PEER_HALF = 128
PEER_TOPK = 16
PEER_SLOTS = PEER_HEADS * PEER_TOPK
MASKED = -1e30

VMEM_LIMIT_BYTES = 62 * 1024 * 1024

IN_PROJ_ROWS = 512
ROUTE_ROWS = 256
EXPERT_ROWS = 512
EXPERT_TILE = 1024
W_PITCH = EXPERT_ROWS + 8
BUILD_GROUP = 16

CAND_ROWS = 80


def _rms(x, g):
    return x * lax.rsqrt(jnp.mean(x * x, axis=-1, keepdims=True) + EPS) * g


def _params(*sem):
    return pltpu.CompilerParams(dimension_semantics=sem, vmem_limit_bytes=VMEM_LIMIT_BYTES)


def _in_proj_kernel(x_ref, g_ref, wpkv_ref, wqt_ref, wg_ref,
                    p_ref, qt_ref, k_ref, v_ref, gate_ref, kmean_ref):
    h = _rms(x_ref[...], g_ref[...]).astype(BF16)
    pkv = jnp.dot(h, wpkv_ref[...], preferred_element_type=F32)
    p_ref[...] = pkv[:, :POOL_WIDTH]
    k = pkv[:, POOL_WIDTH:POOL_WIDTH + ATTN_WIDTH]
    k_ref[...] = k.astype(BF16)
    v_ref[...] = pkv[:, POOL_WIDTH + ATTN_WIDTH:].astype(BF16)
    nb = k.shape[0] // MOBA_BLOCK
    kmean_ref[...] = (jnp.sum(k.reshape(nb, MOBA_BLOCK, ATTN_WIDTH), axis=1) / MOBA_BLOCK
                      ).reshape(nb, 1, ATTN_WIDTH)
    qt = lax.dot_general(wqt_ref[...], h, (((1,), (1,)), ((), ())), preferred_element_type=F32)
    qt_ref[...] = (qt * (HEAD_DIM ** -0.5)).astype(BF16)
    gl = jnp.dot(h, wg_ref[...], preferred_element_type=F32)
    gate_ref[...] = (1.0 / (1.0 + jnp.exp(-gl))).astype(BF16)


def _in_proj(x, g, wpkv, wqt, wg):
    n, d = x.shape
    rows = IN_PROJ_ROWS
    nb = rows // MOBA_BLOCK
    const = lambda t: (0, 0)
    return pl.pallas_call(
        _in_proj_kernel,
        grid=(n // rows,),
        in_specs=[
            pl.BlockSpec((rows, d), lambda t: (t, 0)),
            pl.BlockSpec((1, d), const),
            pl.BlockSpec(wpkv.shape, const),
            pl.BlockSpec(wqt.shape, const),
            pl.BlockSpec(wg.shape, const),
        ],
        out_specs=[
            pl.BlockSpec((rows, POOL_WIDTH), lambda t: (t, 0)),
            pl.BlockSpec((ATTN_WIDTH, rows), lambda t: (0, t)),
            pl.BlockSpec((rows, ATTN_WIDTH), lambda t: (t, 0)),
            pl.BlockSpec((rows, ATTN_WIDTH), lambda t: (t, 0)),
            pl.BlockSpec((rows, wg.shape[1]), lambda t: (t, 0)),
            pl.BlockSpec((nb, 1, ATTN_WIDTH), lambda t: (t, 0, 0)),
        ],
        out_shape=[
            jax.ShapeDtypeStruct((n, POOL_WIDTH), F32),
            jax.ShapeDtypeStruct((ATTN_WIDTH, n), BF16),
            jax.ShapeDtypeStruct((n, ATTN_WIDTH), BF16),
            jax.ShapeDtypeStruct((n, ATTN_WIDTH), BF16),
            jax.ShapeDtypeStruct((n, wg.shape[1]), BF16),
            jax.ShapeDtypeStruct((n // MOBA_BLOCK, 1, ATTN_WIDTH), F32),
        ],
        compiler_params=_params("parallel"),
        name="in_proj",
    )(x, g, wpkv, wqt, wg)


def _mixer_kernel(qt_ref, k_ref, v_ref, kmean_ref, p_ref, pprev_ref, gate_ref, x_ref,
                  wpool_ref, pscale_ref, wup_ref, wout_ref, o_ref,
                  sel_s, m_s, l_s, acc_s, pwin_s):
    blk = MOBA_BLOCK
    i = pl.program_id(1)
    n_blocks = kmean_ref.shape[1]
    slopes = [2.0 ** (-8.0 * (h + 1) / N_HEADS) for h in range(N_HEADS)]

    kmean = kmean_ref[0]
    jrow = lax.broadcasted_iota(jnp.int32, (n_blocks, blk), 0)
    for h in range(N_HEADS):
        hs = slice(h * HEAD_DIM, (h + 1) * HEAD_DIM)
        bs = jnp.dot(kmean[:, hs].astype(BF16), qt_ref[hs, :], preferred_element_type=F32)
        sel = jnp.zeros((n_blocks, blk), F32)
        for j0 in range(n_blocks):
            ref_row = bs[j0:j0 + 1, :]
            ahead = ((bs > ref_row) | ((bs == ref_row) & (jrow < j0))) & (jrow < i)
            cnt = jnp.sum(ahead.astype(F32), axis=0, keepdims=True)
            ok = (cnt < MOBA_TOPK) & (j0 < i)
            sel = jnp.where(jrow == j0, jnp.where(ok, 0.0, MASKED), sel)
        sel_s[h * n_blocks:(h + 1) * n_blocks, :] = sel

    dist0 = (lax.broadcasted_iota(jnp.int32, (blk, blk), 1)
             - lax.broadcasted_iota(jnp.int32, (blk, blk), 0)).astype(F32)

    own = pl.multiple_of(i * blk, blk)
    k_own = k_ref[pl.ds(own, blk), :]
    v_own = v_ref[pl.ds(own, blk), :]
    for h in range(N_HEADS):
        hs = slice(h * HEAD_DIM, (h + 1) * HEAD_DIM)
        s = jnp.dot(k_own[:, hs], qt_ref[hs, :], preferred_element_type=F32)
        s = jnp.where(dist0 >= 0, s - slopes[h] * dist0, MASKED)
        m = jnp.max(s, axis=0, keepdims=True)
        e = jnp.exp(s - m)
        m_s[h:h + 1, :] = m
        l_s[h:h + 1, :] = jnp.sum(e, axis=0, keepdims=True)
        acc_s[hs, :] = lax.dot_general(v_own[:, hs], e.astype(BF16), (((0,), (0,)), ((), ())),
                                       preferred_element_type=F32)

    def past(j, carry):
        start = pl.multiple_of(j * blk, blk)
        kj = k_ref[pl.ds(start, blk), :]
        vj = v_ref[pl.ds(start, blk), :]
        gap = ((i - j) * blk).astype(F32)
        for h in range(N_HEADS):
            hs = slice(h * HEAD_DIM, (h + 1) * HEAD_DIM)
            s = jnp.dot(kj[:, hs], qt_ref[hs, :], preferred_element_type=F32)
            row_bias = sel_s[pl.ds(h * n_blocks + j, 1), :] - slopes[h] * gap
            s = s - slopes[h] * dist0 + row_bias
            m_old = m_s[h:h + 1, :]
            m_new = jnp.maximum(m_old, jnp.max(s, axis=0, keepdims=True))
            alpha = jnp.exp(m_old - m_new)
            e = jnp.exp(s - m_new)
            m_s[h:h + 1, :] = m_new
            l_s[h:h + 1, :] = alpha * l_s[h:h + 1, :] + jnp.sum(e, axis=0, keepdims=True)
            acc_s[hs, :] = alpha * acc_s[hs, :] + lax.dot_general(
                vj[:, hs], e.astype(BF16), (((0,), (0,)), ((), ())), preferred_element_type=F32)
        return carry

    lax.fori_loop(0, i, past, 0)

    for h in range(N_HEADS):
        hs = slice(h * HEAD_DIM, (h + 1) * HEAD_DIM)
        acc_s[hs, :] = acc_s[hs, :] / l_s[h:h + 1, :]
    attn = acc_s[...].T.astype(BF16)
    branch_b = jnp.dot(attn, wup_ref[...], preferred_element_type=F32)

    p = p_ref[...]
    pwin_s[POOL_HALO:, :] = p
    pwin_s[:POOL_HALO, :] = jnp.where(i > 0, pprev_ref[...], 0.0)
    pos = lax.broadcasted_iota(jnp.int32, (blk, 1), 0) + i * blk
    pooled = []
    for gi, w in enumerate(POOL_WINDOWS):
        cs = slice(gi * POOL_GROUP, (gi + 1) * POOL_GROUP)
        tot = p[:, cs]
        for d in range(1, w):
            tot = tot + pwin_s[POOL_HALO - d:POOL_HALO - d + blk, cs]
        cnt = jnp.minimum(pos + 1, w).astype(F32)
        mdiff = tot / cnt - p[:, cs]
        pooled.append(jnp.dot(mdiff.astype(BF16), wpool_ref[gi], preferred_element_type=F32))
    branch_a = jnp.concatenate(pooled, axis=1) * pscale_ref[...]

    gates = gate_ref[...]
    d_model = branch_a.shape[1]
    merged = gates[:, :d_model].astype(F32) * branch_a + gates[:, d_model:].astype(F32) * branch_b
    o_ref[...] = x_ref[...] + jnp.dot(merged.astype(BF16), wout_ref[...], preferred_element_type=F32)


def _mixer(qt, k, v, kmean, p, gates, x, wpool, pscale, wup, wout, batch, seq):
    n, d = x.shape
    blk = MOBA_BLOCK
    nb = seq // blk
    halo_per_blk = blk // POOL_HALO
    row = lambda b, i: (b * nb + i, 0)
    const2 = lambda b, i: (0, 0)
    return pl.pallas_call(
        _mixer_kernel,
        grid=(batch, nb),
        in_specs=[
            pl.BlockSpec((ATTN_WIDTH, blk), lambda b, i: (0, b * nb + i)),
            pl.BlockSpec((seq, ATTN_WIDTH), lambda b, i: (b, 0)),
            pl.BlockSpec((seq, ATTN_WIDTH), lambda b, i: (b, 0)),
            pl.BlockSpec((1, nb, ATTN_WIDTH), lambda b, i: (b, 0, 0)),
            pl.BlockSpec((blk, POOL_WIDTH), row),
            pl.BlockSpec((POOL_HALO, POOL_WIDTH),
                         lambda b, i: (jnp.maximum((b * nb + i) * halo_per_blk - 1, 0), 0)),
            pl.BlockSpec((blk, gates.shape[1]), row),
            pl.BlockSpec((blk, d), row),
            pl.BlockSpec(wpool.shape, lambda b, i: (0, 0, 0)),
            pl.BlockSpec((1, d), const2),
            pl.BlockSpec(wup.shape, const2),
            pl.BlockSpec(wout.shape, const2),
        ],
        out_specs=pl.BlockSpec((blk, d), row),
        out_shape=jax.ShapeDtypeStruct((n, d), F32),
        scratch_shapes=[
            pltpu.VMEM((N_HEADS * nb, blk), F32),
            pltpu.VMEM((N_HEADS, blk), F32),
            pltpu.VMEM((N_HEADS, blk), F32),
            pltpu.VMEM((ATTN_WIDTH, blk), F32),
            pltpu.VMEM((POOL_HALO + blk, POOL_WIDTH), F32),
        ],
        compiler_params=_params("parallel", "arbitrary"),
        name="mixer",
    )(qt, k, v, kmean.reshape(batch, nb, ATTN_WIDTH), p, p, gates, x, wpool, pscale, wup, wout)


def _extract_top(vals, ids, count):
    rows = lax.broadcasted_iota(jnp.int32, (count, vals.shape[1]), 0)
    top_v = jnp.zeros((count, vals.shape[1]), F32)
    top_i = jnp.zeros((count, vals.shape[1]), F32)
    for r in range(count):
        m = jnp.max(vals, axis=0, keepdims=True)
        pick = jnp.min(jnp.where(vals == m, ids, 1e9), axis=0, keepdims=True)
        top_v = jnp.where(rows == r, m, top_v)
        top_i = jnp.where(rows == r, pick, top_i)
        vals = jnp.where(ids == pick, -jnp.inf, vals)
    return top_v, top_i


def _take_rows(table, idx):
    out = jnp.zeros(idx.shape, F32)
    for a in range(table.shape[0]):
        out = jnp.where(idx == a, table[a:a + 1, :], out)
    return out


def _peer_route_kernel(x_ref, g_ref, wq_ref, keys_ref, cid_ref, cok_ref,
                       hn_ref, i_ref, j_ref, gw_ref):
    hn = _rms(x_ref[...], g_ref[...]).astype(BF16)
    hn_ref[...] = hn
    q = jnp.dot(hn, wq_ref[...], preferred_element_type=F32)
    t = q.shape[0]
    key_ids = lax.broadcasted_iota(jnp.int32, (PEER_NKEYS, t), 0).astype(F32)
    cid = cid_ref[...]
    cok = cok_ref[...]
    i_rows, j_rows, g_rows = [], [], []
    for h in range(PEER_HEADS):
        sv, si = [], []
        for half in range(2):
            c = (h * 2 + half) * PEER_HALF
            s = lax.dot_general(keys_ref[half], q[:, c:c + PEER_HALF].astype(BF16),
                                (((1,), (1,)), ((), ())), preferred_element_type=F32)
            top_v, top_i = _extract_top(s, key_ids, PEER_TOPK)
            sv.append(top_v)
            si.append(top_i)
        pieces = [sv[0][0:1, :] + sv[1]]
        for a in range(1, 8):
            pieces.append(sv[0][a:a + 1, :] + sv[1][0:8, :])
        pieces.append(sv[0][8:16, :] + sv[1][0:1, :])
        cand = jnp.where(cok > 0, jnp.concatenate(pieces, axis=0), -jnp.inf)
        cv, cpick = _extract_top(cand, cid, PEER_TOPK)
        a_idx = jnp.floor(cpick * (1.0 / PEER_TOPK))
        b_idx = cpick - a_idx * PEER_TOPK
        i_rows.append(_take_rows(si[0], a_idx))
        j_rows.append(_take_rows(si[1], b_idx))
        e = jnp.exp(cv - cv[0:1, :])
        g_rows.append(e / jnp.sum(e, axis=0, keepdims=True))
    i_ref[...] = jnp.concatenate(i_rows, axis=0).T
    j_ref[...] = jnp.concatenate(j_rows, axis=0).T
    gw_ref[...] = jnp.concatenate(g_rows, axis=0).T


def _candidate_tables(t):
    cid = np.zeros((CAND_ROWS,), np.float32)
    ok = np.zeros((CAND_ROWS,), np.float32)
    r = 0
    for a in range(8):
        width = 16 if a == 0 else 8
        for b in range(width):
            cid[r] = a * PEER_TOPK + b
            ok[r] = 1.0 if (a + 1) * (b + 1) <= PEER_TOPK else 0.0
            r += 1
    for a in range(8, 16):
        cid[r] = a * PEER_TOPK
        ok[r] = 1.0
        r += 1
    assert r == CAND_ROWS
    tile = lambda c: jnp.asarray(np.ascontiguousarray(np.broadcast_to(c[:, None], (CAND_ROWS, t))))
    return tile(cid), tile(ok)


def _peer_route(x, g, wq, keys):
    n, d = x.shape
    rows = ROUTE_ROWS
    cid, cok = _candidate_tables(rows)
    const = lambda t: (0, 0)
    row = lambda t: (t, 0)
    return pl.pallas_call(
        _peer_route_kernel,
        grid=(n // rows,),
        in_specs=[
            pl.BlockSpec((rows, d), row),
            pl.BlockSpec((1, d), const),
            pl.BlockSpec(wq.shape, const),
            pl.BlockSpec(keys.shape, lambda t: (0, 0, 0)),
            pl.BlockSpec(cid.shape, const),
            pl.BlockSpec(cok.shape, const),
        ],
        out_specs=[
            pl.BlockSpec((rows, d), row),
            pl.BlockSpec((rows, PEER_SLOTS), row),
            pl.BlockSpec((rows, PEER_SLOTS), row),
            pl.BlockSpec((rows, PEER_SLOTS), row),
        ],
        out_shape=[
            jax.ShapeDtypeStruct((n, d), BF16),
            jax.ShapeDtypeStruct((n, PEER_SLOTS), F32),
            jax.ShapeDtypeStruct((n, PEER_SLOTS), F32),
            jax.ShapeDtypeStruct((n, PEER_SLOTS), F32),
        ],
        compiler_params=_params("parallel"),
        name="peer_route",
    )(x, g, wq, keys, cid, cok)


def _gelu_tanh(x):
    return 0.5 * x * (1.0 + jnp.tanh(0.7978845608028654 * (x + 0.044715 * (x * x * x))))


def _peer_experts_kernel(hn_ref, i_ref, j_ref, gw_ref, x_ref, ut_ref, v_ref, gfin_ref,
                         o_ref, w_s, *, final_norm):
    e = pl.program_id(1)
    rows = hn_ref.shape[0]
    rows_per_tile = ut_ref.shape[1] // PEER_NKEYS

    @pl.when(e == 0)
    def _build_weights():
        sub = lax.broadcasted_iota(jnp.int32, (PEER_NKEYS, PEER_SLOTS), 0).astype(F32)

        def token_group(tg, carry):
            t0 = pl.multiple_of(tg * BUILD_GROUP, BUILD_GROUP)
            o_ref[pl.ds(t0, BUILD_GROUP), :] = x_ref[pl.ds(t0, BUILD_GROUP), :]
            i_blk = i_ref[pl.ds(t0, BUILD_GROUP), :]
            j_blk = j_ref[pl.ds(t0, BUILD_GROUP), :]
            g_blk = gw_ref[pl.ds(t0, BUILD_GROUP), :]
            for r in range(BUILD_GROUP):
                first = jnp.where(sub == i_blk[r:r + 1, :], g_blk[r:r + 1, :], 0.0).astype(BF16)
                second = jnp.where(sub == j_blk[r:r + 1, :], 1.0, 0.0).astype(BF16)
                w = lax.dot_general(first, second, (((1,), (1,)), ((), ())),
                                    preferred_element_type=F32)
                w_s[pl.ds(t0 + r, PEER_NKEYS, stride=W_PITCH), :] = w
            return carry

        lax.fori_loop(0, rows // BUILD_GROUP, token_group, 0)

    act = _gelu_tanh(jnp.dot(hn_ref[...], ut_ref[...], preferred_element_type=F32))
    base = e * rows_per_tile
    w = jnp.concatenate(
        [w_s[pl.ds(pl.multiple_of((base + r) * W_PITCH, 8), rows), :] for r in range(rows_per_tile)],
        axis=1)
    o_ref[...] += jnp.dot((act * w).astype(BF16), v_ref[...], preferred_element_type=F32)

    if final_norm:
        @pl.when(e == pl.num_programs(1) - 1)
        def _finish():
            o_ref[...] = _rms(o_ref[...], gfin_ref[...])


def _peer_experts(hn, i_idx, j_idx, gw, x, ut, v, gfin, final_norm):
    n, d = x.shape
    rows = EXPERT_ROWS
    tile = EXPERT_TILE
    n_experts = v.shape[0]
    row = lambda t, e: (t, 0)
    once = pl.Buffered(1)
    return pl.pallas_call(
        functools.partial(_peer_experts_kernel, final_norm=final_norm),
        grid=(n // rows, n_experts // tile),
        in_specs=[
            pl.BlockSpec((rows, d), row),
            pl.BlockSpec((rows, PEER_SLOTS), row, pipeline_mode=once),
            pl.BlockSpec((rows, PEER_SLOTS), row, pipeline_mode=once),
            pl.BlockSpec((rows, PEER_SLOTS), row, pipeline_mode=once),
            pl.BlockSpec((rows, d), row, pipeline_mode=once),
            pl.BlockSpec((d, tile), lambda t, e: (0, e)),
            pl.BlockSpec((tile, d), lambda t, e: (e, 0)),
            pl.BlockSpec((1, d), lambda t, e: (0, 0)),
        ],
        out_specs=pl.BlockSpec((rows, d), row),
        out_shape=jax.ShapeDtypeStruct((n, d), F32),
        scratch_shapes=[pltpu.VMEM((PEER_NKEYS * W_PITCH, PEER_NKEYS), F32)],
        compiler_params=_params("parallel", "arbitrary"),
        name="peer_experts",
    )(hn, i_idx, j_idx, gw, x, ut, v, gfin)


def kernel(x, mix_norm, w_in, w_pool, pool_scale, w_attn_up, w_out, ffn_norm, peer_wq, peer_keys,
           peer_u, peer_v, final_norm):
    batch, seq, d = x.shape
    depth = w_in.shape[0]
    assert seq % MOBA_BLOCK == 0 and (batch * seq) % IN_PROJ_ROWS == 0
    xf = x.reshape(batch * seq, d)
    o_q, o_k, o_v, o_g = POOL_WIDTH, POOL_WIDTH + ATTN_WIDTH, POOL_WIDTH + 2 * ATTN_WIDTH, \
        POOL_WIDTH + 3 * ATTN_WIDTH
    for l in range(depth):
        wl = w_in[l]
        wpkv = jnp.concatenate([wl[:, :o_q], wl[:, o_k:o_g]], axis=1).astype(BF16)
        wqt = wl[:, o_q:o_k].T.astype(BF16)
        wg = wl[:, o_g:].astype(BF16)
        p, qt, k, v, gates, kmean = _in_proj(xf, mix_norm[l][None, :], wpkv, wqt, wg)
        xf = _mixer(qt, k, v, kmean, p, gates, xf, w_pool[l].astype(BF16), pool_scale[l][None, :],
                    w_attn_up[l].astype(BF16), w_out[l].astype(BF16), batch, seq)
        hn, i_idx, j_idx, gw = _peer_route(xf, ffn_norm[l][None, :], peer_wq[l].astype(BF16),
                                           peer_keys[l].astype(BF16))
        xf = _peer_experts(hn, i_idx, j_idx, gw, xf, peer_u[l].T.astype(BF16),
                           peer_v[l].astype(BF16), final_norm[None, :], l == depth - 1)
    return xf.reshape(batch, seq, d)
```

```python
import functools

import jax
import jax.numpy as jnp
import numpy as np
from jax import lax
from jax.experimental import pallas as pl
from jax.experimental.pallas import tpu as pltpu

F32 = jnp.float32
BF16 = jnp.bfloat16

EPS = 1e-6
POOL_WINDOWS = (2, 4, 8, 16)
POOL_GROUP = 128
POOL_OUT_GROUP = 256
POOL_WIDTH = 512
POOL_HALO = 16
N_HEADS = 8
HEAD_DIM = 64
HEAD_PAD = 128
ATTN_WIDTH = N_HEADS * HEAD_DIM
ATTN_PAD = N_HEADS * HEAD_PAD
MOBA_BLOCK = 256
MOBA_TOPK = 3
KEY_CHUNK = 128
PEER_HEADS = 8
PEER_NKEYS = 128
PEER_HALF = 128
PEER_TOPK = 16
PEER_SLOTS = PEER_HEADS * PEER_TOPK
MASKED = -1e30

VMEM_LIMIT_BYTES = 62 * 1024 * 1024

IN_PROJ_ROWS = 512
ROUTE_ROWS = 128
EXPERT_ROWS = 512
EXPERT_TILE = 1024
W_PITCH = EXPERT_ROWS + 8
BUILD_GROUP = 32

CAND_ROWS = 80

ALIBI_SLOPES = tuple(2.0 ** (-8.0 * (h + 1) / N_HEADS) for h in range(N_HEADS))


def _rms(x, g):
    return x * lax.rsqrt(jnp.mean(x * x, axis=-1, keepdims=True) + EPS) * g


def _params(*sem):
    return pltpu.CompilerParams(dimension_semantics=sem, vmem_limit_bytes=VMEM_LIMIT_BYTES)


def _in_proj_kernel(x_ref, g_ref, wpg_ref, wk_ref, wqvt_ref, qadd_ref,
                    p_ref, qt_ref, k_ref, vt_ref, gate_ref, kmean_ref):
    rows = x_ref.shape[0]
    nb = rows // MOBA_BLOCK
    h = _rms(x_ref[...], g_ref[...]).astype(BF16)
    pg = jnp.dot(h, wpg_ref[...], preferred_element_type=F32)
    p_ref[...] = pg[:, :POOL_WIDTH]
    gate_ref[...] = (1.0 / (1.0 + jnp.exp(-pg[:, POOL_WIDTH:]))).astype(BF16)

    k = jnp.dot(h, wk_ref[...], preferred_element_type=F32)
    kmean_ref[...] = (jnp.sum(k.reshape(nb, MOBA_BLOCK, ATTN_PAD), axis=1) / MOBA_BLOCK
                      ).reshape(nb, 1, ATTN_PAD)
    offset = (lax.broadcasted_iota(jnp.int32, (rows, 1), 0) & (MOBA_BLOCK - 1)).astype(F32)
    lane = lax.broadcasted_iota(jnp.int32, (1, ATTN_PAD), 1)
    bias_lane = jnp.where((lane & (HEAD_PAD - 1)) == HEAD_DIM, 1.0, 0.0)
    k_ref[...] = (k + offset * bias_lane).astype(BF16)

    qvt = lax.dot_general(wqvt_ref[...], h, (((1,), (1,)), ((), ())), preferred_element_type=F32)
    qt_ref[...] = (qvt[:ATTN_PAD] * (HEAD_DIM ** -0.5) + qadd_ref[...]).astype(BF16)
    vt = qvt[ATTN_PAD:].astype(BF16)
    for c in range(rows // KEY_CHUNK):
        vt_ref[c] = vt[:, c * KEY_CHUNK:(c + 1) * KEY_CHUNK]


def _in_proj(x, g, wpg, wk, wqvt, qadd):
    n, d = x.shape
    rows = IN_PROJ_ROWS
    nb = rows // MOBA_BLOCK
    gate_width = wpg.shape[1] - POOL_WIDTH
    const = lambda t: (0, 0)
    return pl.pallas_call(
        _in_proj_kernel,
        grid=(n // rows,),
        in_specs=[
            pl.BlockSpec((rows, d), lambda t: (t, 0)),
            pl.BlockSpec((1, d), const),
            pl.BlockSpec(wpg.shape, const),
            pl.BlockSpec(wk.shape, const),
            pl.BlockSpec(wqvt.shape, const),
            pl.BlockSpec(qadd.shape, const),
        ],
        out_specs=[
            pl.BlockSpec((rows, POOL_WIDTH), lambda t: (t, 0)),
            pl.BlockSpec((ATTN_PAD, rows), lambda t: (0, t)),
            pl.BlockSpec((rows, ATTN_PAD), lambda t: (t, 0)),
            pl.BlockSpec((rows // KEY_CHUNK, ATTN_WIDTH, KEY_CHUNK), lambda t: (t, 0, 0)),
            pl.BlockSpec((rows, gate_width), lambda t: (t, 0)),
            pl.BlockSpec((nb, 1, ATTN_PAD), lambda t: (t, 0, 0)),
        ],
        out_shape=[
            jax.ShapeDtypeStruct((n, POOL_WIDTH), F32),
            jax.ShapeDtypeStruct((ATTN_PAD, n), BF16),
            jax.ShapeDtypeStruct((n, ATTN_PAD), BF16),
            jax.ShapeDtypeStruct((n // KEY_CHUNK, ATTN_WIDTH, KEY_CHUNK), BF16),
            jax.ShapeDtypeStruct((n, gate_width), BF16),
            jax.ShapeDtypeStruct((n // MOBA_BLOCK, 1, ATTN_PAD), F32),
        ],
        compiler_params=_params("parallel"),
        name="in_proj",
    )(x, g, wpg, wk, wqvt, qadd)


def _mixer_kernel(qt_ref, k_ref, vt_ref, kmean_ref, p_ref, pprev_ref, gate_ref, x_ref,
                  wpool_ref, pscale_ref, wup_ref, wout_ref, o_ref,
                  sel_s, m_s, l_s, acc_s, pwin_s, score_s):
    blk = MOBA_BLOCK
    chunks = blk // KEY_CHUNK
    i = pl.program_id(1)
    n_blocks = kmean_ref.shape[1]
    head_cols = [slice(h * HEAD_PAD, (h + 1) * HEAD_PAD) for h in range(N_HEADS)]
    head_rows = [slice(h * HEAD_DIM, (h + 1) * HEAD_DIM) for h in range(N_HEADS)]
    q_off = lax.broadcasted_iota(jnp.int32, (1, blk), 1).astype(F32)

    kmean = kmean_ref[0]
    jrow = lax.broadcasted_iota(jnp.int32, (n_blocks, blk), 0)
    for h in range(N_HEADS):
        bs = jnp.dot(kmean[:, head_cols[h]].astype(BF16), qt_ref[head_cols[h], :],
                     preferred_element_type=F32)
        sel = jnp.zeros((n_blocks, blk), F32)
        for j0 in range(n_blocks):
            ref_row = bs[j0:j0 + 1, :]
            ahead = ((bs > ref_row) | ((bs == ref_row) & (jrow < j0))) & (jrow < i)
            cnt = jnp.sum(ahead.astype(F32), axis=0, keepdims=True)
            ok = (cnt < MOBA_TOPK) & (j0 < i)
            sel = jnp.where(jrow == j0, jnp.where(ok, 0.0, MASKED), sel)
        sel_s[h * n_blocks:(h + 1) * n_blocks, :] = sel

    def attend_chunk(k_chunk, vt_chunk, finish_logits, first):
        for h in range(N_HEADS):
            score_s[h] = jnp.dot(k_chunk[:, head_cols[h]], qt_ref[head_cols[h], :],
                                 preferred_element_type=F32)
        new_state = []
        for h in range(N_HEADS):
            s = finish_logits(h, score_s[h])
            m_chunk = jnp.max(s, axis=0, keepdims=True)
            if first:
                m_new = m_chunk
            else:
                m_old = m_s[h]
                m_new = jnp.maximum(m_old, m_chunk)
                alpha = jnp.exp(m_old - m_new)
            e = jnp.exp(s - m_new)
            pv = jnp.dot(vt_chunk[head_rows[h], :], e.astype(BF16), preferred_element_type=F32)
            l_chunk = jnp.sum(e, axis=0, keepdims=True)
            if first:
                new_state.append((m_new, l_chunk, pv))
            else:
                new_state.append((m_new, alpha * l_s[h] + l_chunk,
                                  alpha * acc_s[head_rows[h], :] + pv))
        for h, (m_new, l_new, acc_new) in enumerate(new_state):
            m_s[h] = m_new
            l_s[h] = l_new
            acc_s[head_rows[h], :] = acc_new

    own = pl.multiple_of(i * blk, blk)
    for c in range(chunks):
        k_chunk = k_ref[pl.ds(own + c * KEY_CHUNK, KEY_CHUNK), :]
        vt_chunk = vt_ref[i * chunks + c]
        k_off = (lax.broadcasted_iota(jnp.int32, (KEY_CHUNK, 1), 0) + c * KEY_CHUNK).astype(F32)
        visible = q_off >= k_off
        attend_chunk(k_chunk, vt_chunk,
                     lambda h, s: jnp.where(visible, s - ALIBI_SLOPES[h] * q_off, MASKED),
                     first=(c == 0))

    for j in range(n_blocks - 1):
        @pl.when(j < i)
        def _past_block(j=j):
            gap = ((i - j) * blk).astype(F32)
            bias = [sel_s[h * n_blocks + j:h * n_blocks + j + 1, :]
                    - ALIBI_SLOPES[h] * (gap + q_off) for h in range(N_HEADS)]
            for c in range(chunks):
                step = j * chunks + c
                attend_chunk(k_ref[step * KEY_CHUNK:(step + 1) * KEY_CHUNK, :], vt_ref[step],
                             lambda h, s: s + bias[h], first=False)

    for h in range(N_HEADS):
        acc_s[head_rows[h], :] = acc_s[head_rows[h], :] / l_s[h]
    attn = acc_s[...].T.astype(BF16)
    branch_b = jnp.dot(attn, wup_ref[...], preferred_element_type=F32)

    p = p_ref[...]
    pwin_s[POOL_HALO:, :] = p
    pwin_s[:POOL_HALO, :] = jnp.where(i > 0, pprev_ref[...], 0.0)
    pos = lax.broadcasted_iota(jnp.int32, (blk, 1), 0) + i * blk
    pooled = []
    for gi, w in enumerate(POOL_WINDOWS):
        cs = slice(gi * POOL_GROUP, (gi + 1) * POOL_GROUP)
        tot = p[:, cs]
        for d in range(1, w):
            tot = tot + pwin_s[POOL_HALO - d:POOL_HALO - d + blk, cs]
        cnt = jnp.minimum(pos + 1, w).astype(F32)
        mdiff = tot / cnt - p[:, cs]
        pooled.append(jnp.dot(mdiff.astype(BF16), wpool_ref[gi], preferred_element_type=F32))
    branch_a = jnp.concatenate(pooled, axis=1) * pscale_ref[...]

    gates = gate_ref[...]
    d_model = branch_a.shape[1]
    merged = gates[:, :d_model].astype(F32) * branch_a + gates[:, d_model:].astype(F32) * branch_b
    o_ref[...] = x_ref[...] + jnp.dot(merged.astype(BF16), wout_ref[...], preferred_element_type=F32)


def _mixer(qt, k, vt, kmean, p, gates, x, wpool, pscale, wup, wout, batch, seq):
    n, d = x.shape
    blk = MOBA_BLOCK
    nb = seq // blk
    halo_per_blk = blk // POOL_HALO
    row = lambda b, i: (b * nb + i, 0)
    const2 = lambda b, i: (0, 0)
    return pl.pallas_call(
        _mixer_kernel,
        grid=(batch, nb),
        in_specs=[
            pl.BlockSpec((ATTN_PAD, blk), lambda b, i: (0, b * nb + i)),
            pl.BlockSpec((seq, ATTN_PAD), lambda b, i: (b, 0)),
            pl.BlockSpec((seq // KEY_CHUNK, ATTN_WIDTH, KEY_CHUNK), lambda b, i: (b, 0, 0)),
            pl.BlockSpec((1, nb, ATTN_PAD), lambda b, i: (b, 0, 0)),
            pl.BlockSpec((blk, POOL_WIDTH), row),
            pl.BlockSpec((POOL_HALO, POOL_WIDTH),
                         lambda b, i: (jnp.maximum((b * nb + i) * halo_per_blk - 1, 0), 0)),
            pl.BlockSpec((blk, gates.shape[1]), row),
            pl.BlockSpec((blk, d), row),
            pl.BlockSpec(wpool.shape, lambda b, i: (0, 0, 0)),
            pl.BlockSpec((1, d), const2),
            pl.BlockSpec(wup.shape, const2),
            pl.BlockSpec(wout.shape, const2),
        ],
        out_specs=pl.BlockSpec((blk, d), row),
        out_shape=jax.ShapeDtypeStruct((n, d), F32),
        scratch_shapes=[
            pltpu.VMEM((N_HEADS * nb, blk), F32),
            pltpu.VMEM((N_HEADS, 1, blk), F32),
            pltpu.VMEM((N_HEADS, 1, blk), F32),
            pltpu.VMEM((ATTN_WIDTH, blk), F32),
            pltpu.VMEM((POOL_HALO + blk, POOL_WIDTH), F32),
            pltpu.VMEM((N_HEADS, KEY_CHUNK, blk), F32),
        ],
        compiler_params=_params("parallel", "arbitrary"),
        name="mixer",
    )(qt, k, vt, kmean.reshape(batch, nb, ATTN_PAD), p, p, gates, x, wpool, pscale, wup, wout)


def _extract_top(vals, ids, count):
    rows = lax.broadcasted_iota(jnp.int32, (count, vals.shape[1]), 0)
    top_v = jnp.zeros((count, vals.shape[1]), F32)
    top_i = jnp.zeros((count, vals.shape[1]), F32)
    for r in range(count):
        m = jnp.max(vals, axis=0, keepdims=True)
        pick = jnp.min(jnp.where(vals == m, ids, 1e9), axis=0, keepdims=True)
        top_v = jnp.where(rows == r, m, top_v)
        top_i = jnp.where(rows == r, pick, top_i)
        vals = jnp.where(ids == pick, -jnp.inf, vals)
    return top_v, top_i


def _take_rows(table, idx):
    out = jnp.zeros(idx.shape, F32)
    for a in range(table.shape[0]):
        out = jnp.where(idx == a, table[a:a + 1, :], out)
    return out


def _combine_halves(sv, si, cid, cok):
    pieces = [sv[0][0:1, :] + sv[1]]
    for a in range(1, 8):
        pieces.append(sv[0][a:a + 1, :] + sv[1][0:8, :])
    pieces.append(sv[0][8:16, :] + sv[1][0:1, :])
    cand = jnp.where(cok > 0, jnp.concatenate(pieces, axis=0), -jnp.inf)
    cv, cpick = _extract_top(cand, cid, PEER_TOPK)
    a_idx = jnp.floor(cpick * (1.0 / PEER_TOPK))
    b_idx = cpick - a_idx * PEER_TOPK
    e = jnp.exp(cv - cv[0:1, :])
    return _take_rows(si[0], a_idx), _take_rows(si[1], b_idx), e / jnp.sum(e, axis=0, keepdims=True)


def _candidate_tables(t):
    cid = np.zeros((CAND_ROWS,), np.float32)
    ok = np.zeros((CAND_ROWS,), np.float32)
    r = 0
    for a in range(8):
        width = 16 if a == 0 else 8
        for b in range(width):
            cid[r] = a * PEER_TOPK + b
            ok[r] = 1.0 if (a + 1) * (b + 1) <= PEER_TOPK else 0.0
            r += 1
    for a in range(8, 16):
        cid[r] = a * PEER_TOPK
        ok[r] = 1.0
        r += 1
    assert r == CAND_ROWS
    tile = lambda c: jnp.asarray(np.ascontiguousarray(np.broadcast_to(c[:, None], (CAND_ROWS, t))))
    return tile(cid), tile(ok)


def _gelu_tanh(x):
    return 0.5 * x * (1.0 + jnp.tanh(0.7978845608028654 * (x + 0.044715 * (x * x * x))))


def _peer_kernel(xc_ref, xn_ref, gffn_ref, wq_ref, keys_ref, cid_ref, cok_ref, u_ref, v_ref,
                 gfin_ref, o_ref, w_s, hn_s, route_s, slot_s, q_s, score_s, *, final_norm):
    t = pl.program_id(0)
    e = pl.program_id(1)
    rows = o_ref.shape[0]
    half_rows = rows // 2
    rows_per_tile = u_ref.shape[0] // PEER_NKEYS
    nxt = t % 2
    cur = 1 - nxt

    @pl.when((t == 0) & (e == 0))
    def _first_step():
        hn_s[...] = jnp.zeros(hn_s.shape, BF16)
        route_s[...] = jnp.zeros(route_s.shape, F32)

    @pl.when(e == 0)
    def _start_tile():
        hn_next = _rms(xn_ref[...], gffn_ref[...]).astype(BF16)
        hn_s[nxt] = hn_next
        q_all = jnp.dot(hn_next, wq_ref[...], preferred_element_type=F32).astype(BF16)
        for unit in range(q_s.shape[0]):
            q_s[unit] = q_all[:, unit * PEER_HALF:(unit + 1) * PEER_HALF]
        for c in range(3):
            for th in range(2):
                slot_s[c, th * half_rows:(th + 1) * half_rows, :] = route_s[c, th].T
        sub = lax.broadcasted_iota(jnp.int32, (PEER_NKEYS, PEER_SLOTS), 0).astype(F32)

        def token_group(tg, carry):
            t0 = pl.multiple_of(tg * BUILD_GROUP, BUILD_GROUP)
            o_ref[pl.ds(t0, BUILD_GROUP), :] = xc_ref[pl.ds(t0, BUILD_GROUP), :]
            i_blk = slot_s[0, pl.ds(t0, BUILD_GROUP), :]
            j_blk = slot_s[1, pl.ds(t0, BUILD_GROUP), :]
            g_blk = slot_s[2, pl.ds(t0, BUILD_GROUP), :]
            for r in range(BUILD_GROUP):
                first = jnp.where(sub == i_blk[r:r + 1, :], g_blk[r:r + 1, :], 0.0).astype(BF16)
                second = jnp.where(sub == j_blk[r:r + 1, :], 1.0, 0.0).astype(BF16)
                w = lax.dot_general(first, second, (((1,), (1,)), ((), ())),
                                    preferred_element_type=F32)
                w_s[pl.ds(t0 + r, PEER_NKEYS, stride=W_PITCH), :] = w
            return carry

        lax.fori_loop(0, rows // BUILD_GROUP, token_group, 0)

    head = e // 2
    th = e - 2 * head
    row0 = pl.multiple_of(th * half_rows, half_rows)
    for hf in range(2):
        score_s[hf] = lax.dot_general(keys_ref[hf], q_s[2 * head + hf, pl.ds(row0, half_rows), :],
                                      (((1,), (1,)), ((), ())), preferred_element_type=F32)
    key_ids = lax.broadcasted_iota(jnp.int32, (PEER_NKEYS, ROUTE_ROWS), 0).astype(F32)
    routed = [[], [], []]
    for b in range(half_rows // ROUTE_ROWS):
        ts = slice(b * ROUTE_ROWS, (b + 1) * ROUTE_ROWS)
        tops = [_extract_top(score_s[hf, :, ts], key_ids, PEER_TOPK) for hf in range(2)]
        sv = [tops[0][0], tops[1][0]]
        si = [tops[0][1], tops[1][1]]
        for c, val in enumerate(_combine_halves(sv, si, cid_ref[...], cok_ref[...])):
            routed[c].append(val)

    act = _gelu_tanh(lax.dot_general(hn_s[cur], u_ref[...], (((1,), (1,)), ((), ())),
                                     preferred_element_type=F32))
    base = e * rows_per_tile
    w = jnp.concatenate(
        [w_s[pl.ds(pl.multiple_of((base + r) * W_PITCH, 8), rows), :] for r in range(rows_per_tile)],
        axis=1)
    o_ref[...] += jnp.dot((act * w).astype(BF16), v_ref[...], preferred_element_type=F32)

    slots = pl.ds(pl.multiple_of(head * PEER_TOPK, PEER_TOPK), PEER_TOPK)
    for c in range(3):
        route_s[c, th, slots, :] = jnp.concatenate(routed[c], axis=1)

    if final_norm:
        @pl.when(e == pl.num_programs(1) - 1)
        def _finish():
            o_ref[...] = _rms(o_ref[...], gfin_ref[...])


def _peer(x, gffn, wq, keys, u, v, gfin, final_norm):
    n, d = x.shape
    rows = EXPERT_ROWS
    tile = EXPERT_TILE
    n_tiles = n // rows
    n_steps = v.shape[0] // tile
    assert n_steps == 2 * PEER_HEADS, "one routing unit (head, half of the rows) per expert step"
    cid, cok = _candidate_tables(ROUTE_ROWS)
    once = pl.Buffered(1)
    cur_tile = lambda t, e: (jnp.maximum(t - 1, 0), 0)
    const2 = lambda t, e: (0, 0)
    const3 = lambda t, e: (0, 0, 0)
    return pl.pallas_call(
        functools.partial(_peer_kernel, final_norm=final_norm),
        grid=(n_tiles + 1, n_steps),
        in_specs=[
            pl.BlockSpec((rows, d), cur_tile, pipeline_mode=once),
            pl.BlockSpec((rows, d), lambda t, e: (jnp.minimum(t, n_tiles - 1), 0), pipeline_mode=once),
            pl.BlockSpec((1, d), const2),
            pl.BlockSpec(wq.shape, const2, pipeline_mode=once),
            pl.BlockSpec(keys.shape, const3),
            pl.BlockSpec(cid.shape, const2),
            pl.BlockSpec(cok.shape, const2),
            pl.BlockSpec((tile, d), lambda t, e: (e, 0)),
            pl.BlockSpec((tile, d), lambda t, e: (e, 0)),
            pl.BlockSpec((1, d), const2),
        ],
        out_specs=pl.BlockSpec((rows, d), cur_tile),
        out_shape=jax.ShapeDtypeStruct((n, d), F32),
        scratch_shapes=[
            pltpu.VMEM((PEER_NKEYS * W_PITCH, PEER_NKEYS), F32),
            pltpu.VMEM((2, rows, d), BF16),
            pltpu.VMEM((3, 2, PEER_SLOTS, rows // 2), F32),
            pltpu.VMEM((3, rows, PEER_SLOTS), F32),
            pltpu.VMEM((n_steps, rows, PEER_HALF), BF16),
            pltpu.VMEM((2, PEER_NKEYS, rows // 2), F32),
        ],
        compiler_params=_params("arbitrary", "arbitrary"),
        name="peer",
    )(x, x, gffn, wq, keys, cid, cok, u, v, gfin)


def _pad_heads(w):
    d = w.shape[0]
    w = w.reshape(d, N_HEADS, HEAD_DIM)
    return jnp.pad(w, ((0, 0), (0, 0), (0, HEAD_PAD - HEAD_DIM))).reshape(d, ATTN_PAD)


def kernel(x, mix_norm, w_in, w_pool, pool_scale, w_attn_up, w_out, ffn_norm, peer_wq, peer_keys,
           peer_u, peer_v, final_norm):
    batch, seq, d = x.shape
    depth = w_in.shape[0]
    assert seq % MOBA_BLOCK == 0 and (batch * seq) % IN_PROJ_ROWS == 0
    xf = x.reshape(batch * seq, d)
    o_q, o_k, o_v, o_g = POOL_WIDTH, POOL_WIDTH + ATTN_WIDTH, POOL_WIDTH + 2 * ATTN_WIDTH, \
        POOL_WIDTH + 3 * ATTN_WIDTH
    slope_rows = np.zeros((ATTN_PAD, 1), np.float32)
    for h in range(N_HEADS):
        slope_rows[h * HEAD_PAD + HEAD_DIM, 0] = ALIBI_SLOPES[h]
    qadd = jnp.asarray(slope_rows)
    for l in range(depth):
        wl = w_in[l]
        wpg = jnp.concatenate([wl[:, :o_q], wl[:, o_g:]], axis=1).astype(BF16)
        wk = _pad_heads(wl[:, o_k:o_v]).astype(BF16)
        wqvt = jnp.concatenate([_pad_heads(wl[:, o_q:o_k]), wl[:, o_v:o_g]], axis=1).T.astype(BF16)
        p, qt, k, vt, gates, kmean = _in_proj(xf, mix_norm[l][None, :], wpg, wk, wqvt, qadd)
        xf = _mixer(qt, k, vt, kmean, p, gates, xf, w_pool[l].astype(BF16), pool_scale[l][None, :],
                    w_attn_up[l].astype(BF16), w_out[l].astype(BF16), batch, seq)
        xf = _peer(xf, ffn_norm[l][None, :], peer_wq[l].astype(BF16), peer_keys[l].astype(BF16),
                   peer_u[l].astype(BF16), peer_v[l].astype(BF16), final_norm[None, :],
                   l == depth - 1)
    return xf.reshape(batch, seq, d)
```
